```python
import math
import jax, jax.numpy as jnp
from jax import lax
import numpy as np

D_MODEL = 1024
BATCH = 8
SEQ = 8192
DEPTH = 1

CHUNK = 64
Q_BLOCK = 128

MIX_WIDTH = D_MODEL
SB_HEADS = 8
SB_HEAD_DIM = 64
MLA_HEADS = 8
MLA_NOPE_DIM = 64
MLA_ROPE_DIM = 32
MLA_V_DIM = 64
MLA_Q_LORA = 256
MLA_KV_LORA = 128
ROPE_THETA = 10000.0

SB_WIDTH = SB_HEADS * SB_HEAD_DIM
MLA_WIDTH = MLA_HEADS * MLA_V_DIM
IN_SPLITS = (SB_WIDTH, SB_WIDTH, SB_WIDTH, MLA_Q_LORA, MLA_KV_LORA, MLA_ROPE_DIM)
IN_WIDTH = sum(IN_SPLITS)

FFN_HIDDEN = ((8 * D_MODEL // 3 + 255) // 256) * 256
EPS = 1e-6

kernel_name = "hymba_stickbreak_mla_swiglu_block"


def rms_norm(x, g):
    xf = x.astype(jnp.float32)
    y = xf * lax.rsqrt(jnp.mean(xf * xf, axis=-1, keepdims=True) + EPS)
    return (y * g.astype(jnp.float32)).astype(x.dtype)


def rope_cos_sin(positions):
    inv_freq = 1.0 / (ROPE_THETA ** (jnp.arange(0, MLA_ROPE_DIM, 2, dtype=jnp.float32) / MLA_ROPE_DIM))
    ang = positions.astype(jnp.float32)[..., None] * inv_freq
    return jnp.cos(ang), jnp.sin(ang)


def apply_rope(x, cos, sin):
    xf = x.astype(jnp.float32)
    x1, x2 = jnp.split(xf, 2, axis=-1)
    out = jnp.concatenate([x1 * cos - x2 * sin, x2 * cos + x1 * sin], axis=-1)
    return out.astype(x.dtype)


def stick_breaking_block(q, k, v, q_start):
    tq, lk = q.shape[1], k.shape[1]
    z = jnp.einsum('bqhd,bkhd->bhqk', q.astype(jnp.float32), k.astype(jnp.float32)) / math.sqrt(SB_HEAD_DIM)
    t_idx = q_start + jnp.arange(tq)
    s_idx = jnp.arange(lk)
    mask = s_idx[None, :] < t_idx[:, None]
    log_beta = jax.nn.log_sigmoid(z)
    log_1m_beta = jnp.where(mask, jax.nn.log_sigmoid(-z), 0.0)
    suffix = lax.cumsum(log_1m_beta, axis=3, reverse=True) - log_1m_beta
    att = jnp.where(mask, jnp.exp(log_beta + suffix), 0.0)
    return jnp.einsum('bhqk,bkhd->bqhd', att.astype(v.dtype), v)


def mla_block(q_nope, q_pe, k_nope, k_pe, v, q_start):
    tq, lk = q_nope.shape[1], k_nope.shape[1]
    scale = 1.0 / math.sqrt(MLA_NOPE_DIM + MLA_ROPE_DIM)
    s = (jnp.einsum('bqhd,bkhd->bhqk', q_nope.astype(jnp.float32), k_nope.astype(jnp.float32))
         + jnp.einsum('bqhd,bkd->bhqk', q_pe.astype(jnp.float32), k_pe.astype(jnp.float32))) * scale
    q_chunk = (q_start + jnp.arange(tq)) // CHUNK
    k_chunk = jnp.arange(lk) // CHUNK
    mask = k_chunk[None, :] <= q_chunk[:, None]
    s = jnp.where(mask, s, -jnp.inf)
    p = jax.nn.softmax(s, axis=-1)
    return jnp.einsum('bhqk,bkhd->bqhd', p.astype(v.dtype), v)


def setup_inputs(seed: int = 0) -> dict:
    key = jax.random.key(seed)
    ks = jax.random.split(key, 20)

    def w(k, shape):
        return jax.random.normal(k, shape, jnp.float32) * shape[0] ** -0.5

    def gain(k, n):
        return 1.0 + 0.02 * jax.random.normal(k, (n,), jnp.float32)

    x = jax.random.normal(ks[0], (BATCH, SEQ, D_MODEL), jnp.float32)
    offsets = jax.random.randint(ks[1], (BATCH, 1), 0, 4096, dtype=jnp.int32)
    positions = offsets + jnp.arange(SEQ, dtype=jnp.int32)[None, :]
    return {
        "x": x,
        "positions": positions,
        "g_mix_norm": gain(ks[2], D_MODEL),
        "w_in": w(ks[3], (D_MODEL, IN_WIDTH)),
        "g_q_latent": gain(ks[4], MLA_Q_LORA),
        "w_uq": w(ks[5], (MLA_Q_LORA, MLA_HEADS * (MLA_NOPE_DIM + MLA_ROPE_DIM))),
        "g_kv_latent": gain(ks[6], MLA_KV_LORA),
        "w_ukv": w(ks[7], (MLA_KV_LORA, MLA_HEADS * (MLA_NOPE_DIM + MLA_V_DIM))),
        "g_sb_out": gain(ks[8], SB_WIDTH),
        "g_mla_out": gain(ks[9], MLA_WIDTH),
        "w_out": w(ks[10], (MIX_WIDTH, D_MODEL)),
        "g_ffn_norm": gain(ks[11], D_MODEL),
        "w_gate": w(ks[12], (D_MODEL, FFN_HIDDEN)),
        "w_up": w(ks[13], (D_MODEL, FFN_HIDDEN)),
        "w_down": w(ks[14], (FFN_HIDDEN, D_MODEL)),
        "g_final": gain(ks[15], D_MODEL),
    }


def reference(x, positions, g_mix_norm, w_in, g_q_latent, w_uq, g_kv_latent, w_ukv,
              g_sb_out, g_mla_out, w_out, g_ffn_norm, w_gate, w_up, w_down, g_final):
    b, seq, _ = x.shape
    n_blocks = seq // Q_BLOCK
    cos, sin = rope_cos_sin(positions)
    h = x
    for _layer in range(DEPTH):
        hn = rms_norm(h, g_mix_norm)
        proj = hn @ w_in
        cuts = list(np.cumsum(IN_SPLITS)[:-1])
        sb_q, sb_k, sb_v, q_lat, kv_lat, k_rope = jnp.split(proj, cuts, axis=-1)
        sb_q = sb_q.reshape(b, seq, SB_HEADS, SB_HEAD_DIM)
        sb_k = sb_k.reshape(b, seq, SB_HEADS, SB_HEAD_DIM)
        sb_v = sb_v.reshape(b, seq, SB_HEADS, SB_HEAD_DIM)

        q_m = (rms_norm(q_lat, g_q_latent) @ w_uq).reshape(b, seq, MLA_HEADS, MLA_NOPE_DIM + MLA_ROPE_DIM)
        q_nope, q_pe = q_m[..., :MLA_NOPE_DIM], q_m[..., MLA_NOPE_DIM:]
        q_pe = apply_rope(q_pe, cos[:, :, None, :], sin[:, :, None, :])
        kv = (rms_norm(kv_lat, g_kv_latent) @ w_ukv).reshape(b, seq, MLA_HEADS, MLA_NOPE_DIM + MLA_V_DIM)
        k_nope, m_v = kv[..., :MLA_NOPE_DIM], kv[..., MLA_NOPE_DIM:]
        k_pe = apply_rope(k_rope, cos, sin)

        sb_outs, mla_outs = [], []
        for i in range(n_blocks):
            q0, q1 = i * Q_BLOCK, (i + 1) * Q_BLOCK
            sb_outs.append(stick_breaking_block(sb_q[:, q0:q1], sb_k[:, :q1], sb_v[:, :q1], q0))
            mla_outs.append(mla_block(q_nope[:, q0:q1], q_pe[:, q0:q1], k_nope[:, :q1],
                                      k_pe[:, :q1], m_v[:, :q1], q0))
        sb_out = jnp.concatenate(sb_outs, axis=1).reshape(b, seq, SB_WIDTH)
        mla_out = jnp.concatenate(mla_outs, axis=1).reshape(b, seq, MLA_WIDTH)

        mixed = jnp.concatenate([rms_norm(sb_out, g_sb_out), rms_norm(mla_out, g_mla_out)], axis=-1)
        h = h + mixed @ w_out

        fn = rms_norm(h, g_ffn_norm)
        h = h + (jax.nn.silu(fn @ w_gate) * (fn @ w_up)) @ w_down
    return rms_norm(h, g_final)
```

```python
import functools
import math

import jax
import jax.numpy as jnp
from jax import lax
from jax.experimental import pallas as pl
from jax.experimental.pallas import tpu as pltpu

D_MODEL = 1024
N_HEADS = 8
HEAD_DIM = 64
ROPE_DIM = 32
Q_LORA = 256
KV_LORA = 128
GROUP_WIDTH = N_HEADS * HEAD_DIM
N_PAIRS = N_HEADS // 2
CHUNK = 64
ROPE_THETA = 10000.0
EPS = 1e-6
LOG2E = 1.4426950408889634

LANES = 128
MLA_QK_WIDTH = 2 * LANES

TOKEN_TILE = 512
EPILOGUE_TILE = 256
ATT_TILE = 256
VMEM_LIMIT = 56 * 1024 * 1024

F32 = jnp.float32
BF16 = jnp.bfloat16


def _rms(x, g):
    return x * lax.rsqrt(jnp.mean(x * x, axis=-1, keepdims=True) + EPS) * g


def _dot(a, b):
    return jnp.dot(a, b, preferred_element_type=F32)


def _dot_nt(a, b):
    return lax.dot_general(a, b, (((1,), (1,)), ((), ())), preferred_element_type=F32)


def _proj_kernel(x_ref, pos_col_ref, pos_row_ref, g_mix_ref, w_nat_ref, w_t_ref,
                 g_q_ref, w_uq_ref, w_uq_rot_ref, g_kv_ref, w_uv_ref, w_uk_t_ref,
                 invf_row_ref, invf_col_ref,
                 sbq_ref, sbkt_ref, sbv_ref, mq_ref, mkt_ref, mv_ref):
    x = x_ref[0]
    hn = _rms(x, g_mix_ref[...]).astype(BF16)
    nat = _dot(hn, w_nat_ref[...])
    tr = _dot_nt(w_t_ref[...], hn)

    sb_scale = LOG2E / math.sqrt(HEAD_DIM)
    sbq_ref[0] = (nat[:, 0:GROUP_WIDTH] * sb_scale).astype(BF16)
    sbv_ref[0] = nat[:, GROUP_WIDTH:2 * GROUP_WIDTH].astype(BF16)
    sbkt_ref[0] = tr[0:GROUP_WIDTH, :].astype(BF16)

    q_lat = nat[:, 2 * GROUP_WIDTH:2 * GROUP_WIDTH + Q_LORA]
    kv_lat = nat[:, 2 * GROUP_WIDTH + Q_LORA:2 * GROUP_WIDTH + Q_LORA + KV_LORA]
    qn = _rms(q_lat, g_q_ref[...]).astype(BF16)
    kvn = _rms(kv_lat, g_kv_ref[...]).astype(BF16)

    ang = pos_col_ref[0].astype(F32) * invf_row_ref[...]
    cos_q, sin_q = jnp.cos(ang), jnp.sin(ang)
    ang_t = invf_col_ref[...] * pos_row_ref[0].astype(F32)
    cos_k, sin_k = jnp.cos(ang_t), jnp.sin(ang_t)

    q_all = _dot(qn, w_uq_ref[...])
    q_rot = _dot(qn, w_uq_rot_ref[...])
    m_scale = LOG2E / math.sqrt(HEAD_DIM + ROPE_DIM)
    for p in range(N_PAIRS):
        base = p * MLA_QK_WIDTH
        mq_ref[0, :, base:base + LANES] = (q_all[:, base:base + LANES] * m_scale).astype(BF16)
        pe = q_all[:, base + LANES:base + 2 * LANES] * cos_q + q_rot[:, p * LANES:(p + 1) * LANES] * sin_q
        mq_ref[0, :, base + LANES:base + 2 * LANES] = (pe * m_scale).astype(BF16)

    mv_ref[0] = _dot(kvn, w_uv_ref[...]).astype(BF16)
    kn_t = _dot_nt(w_uk_t_ref[...], kvn)
    kpe_t = (tr[GROUP_WIDTH:GROUP_WIDTH + LANES, :] * cos_k
             + tr[GROUP_WIDTH + LANES:GROUP_WIDTH + 2 * LANES, :] * sin_k).astype(BF16)
    for p in range(N_PAIRS):
        base = p * MLA_QK_WIDTH
        mkt_ref[0, base:base + LANES, :] = kn_t[p * LANES:(p + 1) * LANES, :].astype(BF16)
        mkt_ref[0, base + LANES:base + 2 * LANES, :] = kpe_t


def _sb_kernel(q_ref, kt_ref, v_ref, o_ref, acc_ref, c_ref):
    t = ATT_TILE
    i = pl.program_id(2)
    q = q_ref[0]
    lane = lax.broadcasted_iota(jnp.int32, (t, LANES), 1)
    zero = jnp.zeros_like(q)
    q_heads = (jnp.where(lane < HEAD_DIM, q, zero), jnp.where(lane >= HEAD_DIM, q, zero))
    row = lax.broadcasted_iota(jnp.int32, (t, t), 0)
    col = lax.broadcasted_iota(jnp.int32, (t, t), 1)
    later = (row > col).astype(BF16)
    causal = col < row

    def block(j, masked, first):
        start = pl.multiple_of(j * t, t)
        kt = kt_ref[0, :, pl.ds(start, t)]
        v = v_ref[0, pl.ds(start, t), :]
        for h in range(2):
            z = _dot(q_heads[h], kt)
            soft = jnp.log2(1.0 + jnp.exp2(-jnp.abs(z)))
            log_beta = jnp.minimum(z, 0.0) - soft
            log_1m = log_beta - z
            if masked:
                log_1m = jnp.where(causal, log_1m, 0.0)
            hi = log_1m.astype(BF16)
            lo = (log_1m - hi.astype(F32)).astype(BF16)
            suffix = _dot(hi, later) + _dot(lo, later)
            if first:
                arg = log_beta + suffix
            else:
                c = c_ref[h]
                arg = log_beta + suffix + jnp.concatenate([c] * (t // LANES), axis=1)
            att = jnp.exp2(arg)
            if masked:
                att = jnp.where(causal, att, 0.0)
            pv = _dot(att.astype(BF16), v)
            tot = jnp.sum(log_1m, axis=1, keepdims=True)
            if first:
                acc_ref[h] = pv
                c_ref[h] = jnp.broadcast_to(tot, (t, LANES))
            else:
                acc_ref[h] += pv
                c_ref[h] = c_ref[h] + tot

    block(i, True, True)

    def body(s, carry):
        block(i - 1 - s, False, False)
        return carry

    lax.fori_loop(0, i, body, 0)
    o_ref[0] = jnp.where(lane < HEAD_DIM, acc_ref[0], acc_ref[1])


def _mla_kernel(q_ref, kt_ref, v_ref, o_ref, acc_ref, m_ref, l_ref):
    t = ATT_TILE
    i = pl.program_id(2)
    q = q_ref[0]
    qlane = lax.broadcasted_iota(jnp.int32, (t, MLA_QK_WIDTH), 1)
    zero = jnp.zeros_like(q)
    pe0 = LANES
    sel0 = (qlane < HEAD_DIM) | ((qlane >= pe0) & (qlane < pe0 + ROPE_DIM))
    sel1 = ((qlane >= HEAD_DIM) & (qlane < pe0)) | ((qlane >= pe0 + ROPE_DIM) & (qlane < pe0 + 2 * ROPE_DIM))
    q_heads = (jnp.where(sel0, q, zero), jnp.where(sel1, q, zero))
    row = lax.broadcasted_iota(jnp.int32, (t, t), 0)
    col = lax.broadcasted_iota(jnp.int32, (t, t), 1)
    visible = (col // CHUNK) <= (row // CHUNK)
    lane = lax.broadcasted_iota(jnp.int32, (t, LANES), 1)

    def block(j, first):
        start = pl.multiple_of(j * t, t)
        kt = kt_ref[0, :, pl.ds(start, t)]
        v = v_ref[0, pl.ds(start, t), :]
        for h in range(2):
            s = _dot(q_heads[h], kt)
            if first:
                s = jnp.where(visible, s, -jnp.inf)
                m_new = jnp.broadcast_to(jnp.max(s, axis=1, keepdims=True), (t, LANES))
            else:
                m_old = m_ref[h]
                m_new = jnp.maximum(m_old, jnp.max(s, axis=1, keepdims=True))
            p = jnp.exp2(s - jnp.concatenate([m_new] * (t // LANES), axis=1))
            psum = jnp.sum(p, axis=1, keepdims=True)
            pv = _dot(p.astype(BF16), v)
            if first:
                l_ref[h] = jnp.broadcast_to(psum, (t, LANES))
                acc_ref[h] = pv
            else:
                alpha = jnp.exp2(m_old - m_new)
                l_ref[h] = alpha * l_ref[h] + psum
                acc_ref[h] = alpha * acc_ref[h] + pv
            m_ref[h] = m_new

    block(i, True)

    def body(s, carry):
        block(i - 1 - s, False)
        return carry

    lax.fori_loop(0, i, body, 0)
    o_ref[0] = jnp.where(lane < HEAD_DIM, acc_ref[0] / l_ref[0], acc_ref[1] / l_ref[1])


def _epilogue_kernel(x_ref, sb_ref, mla_ref, g_sb_ref, g_mla_ref, w_out_ref, g_ffn_ref,
                     w_gate_ref, w_up_ref, w_down_ref, g_final_ref, o_ref):
    a = _rms(sb_ref[0], g_sb_ref[...]).astype(BF16)
    b = _rms(mla_ref[0], g_mla_ref[...]).astype(BF16)
    h = x_ref[0] + _dot(a, w_out_ref[0:GROUP_WIDTH, :]) + _dot(b, w_out_ref[GROUP_WIDTH:2 * GROUP_WIDTH, :])
    fn = _rms(h, g_ffn_ref[...]).astype(BF16)
    gate = _dot(fn, w_gate_ref[...])
    up = _dot(fn, w_up_ref[...])
    act = (gate / (1.0 + jnp.exp(-gate)) * up).astype(BF16)
    h = h + _dot(act, w_down_ref[...])
    o_ref[0] = _rms(h, g_final_ref[...])


def _const_spec(shape):
    return pl.BlockSpec(shape, lambda *_: (0,) * len(shape), pipeline_mode=pl.Buffered(1))


def _rot_cols(w):
    half = ROPE_DIM // 2
    return jnp.concatenate([-w[..., half:], w[..., :half]], axis=-1)


def kernel(x, positions, g_mix_norm, w_in, g_q_latent, w_uq, g_kv_latent, w_ukv,
           g_sb_out, g_mla_out, w_out, g_ffn_norm, w_gate, w_up, w_down, g_final):
    b, s, d = x.shape
    hidden = w_gate.shape[1]
    assert d == D_MODEL and s % TOKEN_TILE == 0 and s % ATT_TILE == 0 and s % EPILOGUE_TILE == 0

    c0, c1, c2 = GROUP_WIDTH, 2 * GROUP_WIDTH, 3 * GROUP_WIDTH
    c3, c4 = c2 + Q_LORA, c2 + Q_LORA + KV_LORA
    w_nat = jnp.concatenate([w_in[:, 0:c0], w_in[:, c1:c2], w_in[:, c2:c3], w_in[:, c3:c4]], axis=1).astype(BF16)
    w_kr = w_in[:, c4:c4 + ROPE_DIM]
    pad = jnp.zeros((d, LANES - 2 * ROPE_DIM), F32)
    w_t = jnp.concatenate([w_in[:, c0:c1], w_kr, w_kr, pad, _rot_cols(w_kr), _rot_cols(w_kr), pad],
                          axis=1).T.astype(BF16)

    uq = w_uq.reshape(Q_LORA, N_HEADS, HEAD_DIM + ROPE_DIM)
    uq_nope = uq[:, :, :HEAD_DIM].reshape(Q_LORA, N_PAIRS, 2 * HEAD_DIM)
    uq_pe = uq[:, :, HEAD_DIM:].reshape(Q_LORA, N_PAIRS, 2 * ROPE_DIM)
    uq_rot = _rot_cols(uq[:, :, HEAD_DIM:]).reshape(Q_LORA, N_PAIRS, 2 * ROPE_DIM)
    qpad = jnp.zeros((Q_LORA, N_PAIRS, LANES - 2 * ROPE_DIM), F32)
    w_uq_ext = jnp.concatenate([uq_nope, uq_pe, qpad], axis=2).reshape(Q_LORA, N_PAIRS * MLA_QK_WIDTH).astype(BF16)
    w_uq_rot = jnp.concatenate([uq_rot, qpad], axis=2).reshape(Q_LORA, N_PAIRS * LANES).astype(BF16)

    ukv = w_ukv.reshape(KV_LORA, N_HEADS, 2 * HEAD_DIM)
    w_uk_t = ukv[:, :, :HEAD_DIM].reshape(KV_LORA, GROUP_WIDTH).T.astype(BF16)
    w_uv = ukv[:, :, HEAD_DIM:].reshape(KV_LORA, GROUP_WIDTH).astype(BF16)

    inv_freq = 1.0 / (ROPE_THETA ** (jnp.arange(0, ROPE_DIM, 2, dtype=F32) / ROPE_DIM))
    invf = jnp.concatenate([inv_freq] * 4 + [jnp.zeros((LANES - 2 * ROPE_DIM,), F32)])
    invf_row, invf_col = invf.reshape(1, LANES), invf.reshape(LANES, 1)

    pos_col = positions.reshape(b, s, 1)
    pos_row = positions.reshape(b, 1, s)
    row2 = lambda g: g.reshape(1, -1)

    tm = TOKEN_TILE
    tok3 = lambda width: pl.BlockSpec((1, tm, width), lambda bi, ti: (bi, ti, 0))
    tr3 = lambda rows: pl.BlockSpec((1, rows, tm), lambda bi, ti: (bi, 0, ti))
    sbq, sbkt, sbv, mq, mkt, mv = pl.pallas_call(
        _proj_kernel,
        grid=(b, s // tm),
        in_specs=[tok3(d), tok3(1), tr3(1), _const_spec((1, d)), _const_spec(w_nat.shape), _const_spec(w_t.shape),
                  _const_spec((1, Q_LORA)), _const_spec(w_uq_ext.shape), _const_spec(w_uq_rot.shape),
                  _const_spec((1, KV_LORA)), _const_spec(w_uv.shape), _const_spec(w_uk_t.shape),
                  _const_spec((1, LANES)), _const_spec((LANES, 1))],
        out_specs=[tok3(GROUP_WIDTH), tr3(GROUP_WIDTH), tok3(GROUP_WIDTH),
                   tok3(N_PAIRS * MLA_QK_WIDTH), tr3(N_PAIRS * MLA_QK_WIDTH), tok3(GROUP_WIDTH)],
        out_shape=[jax.ShapeDtypeStruct((b, s, GROUP_WIDTH), BF16),
                   jax.ShapeDtypeStruct((b, GROUP_WIDTH, s), BF16),
                   jax.ShapeDtypeStruct((b, s, GROUP_WIDTH), BF16),
                   jax.ShapeDtypeStruct((b, s, N_PAIRS * MLA_QK_WIDTH), BF16),
                   jax.ShapeDtypeStruct((b, N_PAIRS * MLA_QK_WIDTH, s), BF16),
                   jax.ShapeDtypeStruct((b, s, GROUP_WIDTH), BF16)],
        compiler_params=pltpu.CompilerParams(dimension_semantics=("arbitrary", "arbitrary"),
                                             vmem_limit_bytes=VMEM_LIMIT),
    )(x, pos_col, pos_row, row2(g_mix_norm), w_nat, w_t, row2(g_q_latent), w_uq_ext, w_uq_rot,
      row2(g_kv_latent), w_uv, w_uk_t, invf_row, invf_col)

    t = ATT_TILE
    att_params = pltpu.CompilerParams(dimension_semantics=("arbitrary", "arbitrary", "arbitrary"),
                                      vmem_limit_bytes=VMEM_LIMIT)
    q_spec = lambda width: pl.BlockSpec((1, t, width), lambda bi, pi, qi: (bi, qi, pi))
    kt_spec = lambda rows: pl.BlockSpec((1, rows, s), lambda bi, pi, qi: (bi, pi, 0))
    v_spec = pl.BlockSpec((1, s, LANES), lambda bi, pi, qi: (bi, 0, pi))
    o_spec = pl.BlockSpec((1, t, LANES), lambda bi, pi, qi: (bi, qi, pi))
    att_out = jax.ShapeDtypeStruct((b, s, GROUP_WIDTH), F32)
    pair_scratch = pltpu.VMEM((2, t, LANES), F32)

    sb_out = pl.pallas_call(
        _sb_kernel,
        grid=(b, N_PAIRS, s // t),
        in_specs=[q_spec(LANES), kt_spec(LANES), v_spec],
        out_specs=o_spec,
        out_shape=att_out,
        scratch_shapes=[pair_scratch, pair_scratch],
        compiler_params=att_params,
    )(sbq, sbkt, sbv)

    mla_out = pl.pallas_call(
        _mla_kernel,
        grid=(b, N_PAIRS, s // t),
        in_specs=[q_spec(MLA_QK_WIDTH), kt_spec(MLA_QK_WIDTH), v_spec],
        out_specs=o_spec,
        out_shape=att_out,
        scratch_shapes=[pair_scratch, pair_scratch, pair_scratch],
        compiler_params=att_params,
    )(mq, mkt, mv)

    te = EPILOGUE_TILE
    etok = lambda width: pl.BlockSpec((1, te, width), lambda bi, ti: (bi, ti, 0))
    return pl.pallas_call(
        _epilogue_kernel,
        grid=(b, s // te),
        in_specs=[etok(d), etok(GROUP_WIDTH), etok(GROUP_WIDTH), _const_spec((1, GROUP_WIDTH)),
                  _const_spec((1, GROUP_WIDTH)), _const_spec((d, d)), _const_spec((1, d)),
                  _const_spec((d, hidden)), _const_spec((d, hidden)), _const_spec((hidden, d)),
                  _const_spec((1, d))],
        out_specs=etok(d),
        out_shape=jax.ShapeDtypeStruct((b, s, d), F32),
        compiler_params=pltpu.CompilerParams(dimension_semantics=("arbitrary", "arbitrary"),
                                             vmem_limit_bytes=VMEM_LIMIT),
    )(x, sb_out, mla_out, row2(g_sb_out), row2(g_mla_out), w_out.astype(BF16), row2(g_ffn_norm),
      w_gate.astype(BF16), w_up.astype(BF16), w_down.astype(BF16), row2(g_final))
```

```python
import math

import jax
import jax.numpy as jnp
from jax import lax
from jax.experimental import pallas as pl
from jax.experimental.pallas import tpu as pltpu

D_MODEL = 1024
N_HEADS = 8
HEAD_DIM = 64
ROPE_DIM = 32
Q_LORA = 256
KV_LORA = 128
GROUP_WIDTH = N_HEADS * HEAD_DIM
N_PAIRS = N_HEADS // 2
CHUNK = 64
ROPE_THETA = 10000.0
EPS = 1e-6
LOG2E = 1.4426950408889634
MASKED_LOG = -1e30

LANES = 128
MLA_QK_WIDTH = 2 * LANES

TOKEN_TILE = 512
EPILOGUE_TILE = 256
SB_K_TILE = 256
SB_Q_TILE = 2 * SB_K_TILE
SB_ROW_CHUNK = 32
SB_MXU_ROWS = 256
MLA_TILE = 512
VMEM_LIMIT = 56 * 1024 * 1024

F32 = jnp.float32
BF16 = jnp.bfloat16


def _rms(x, g):
    return x * lax.rsqrt(jnp.mean(x * x, axis=-1, keepdims=True) + EPS) * g


def _dot(a, b):
    return jnp.dot(a, b, preferred_element_type=F32)


def _dot_nt(a, b):
    return lax.dot_general(a, b, (((1,), (1,)), ((), ())), preferred_element_type=F32)


def _lane_tile(x, width):
    return jnp.concatenate([x] * (width // LANES), axis=1)


def _by_parity(k, fn):
    par = lax.rem(k, 2)
    pl.when(par == 0)(lambda: fn(0))
    pl.when(par == 1)(lambda: fn(1))


def _proj_kernel(x_ref, pos_col_ref, pos_row_ref, g_mix_ref, w_nat_ref, w_t_ref,
                 g_q_ref, w_uq_ref, w_uq_rot_ref, g_kv_ref, w_uv_ref, w_uk_t_ref,
                 invf_row_ref, invf_col_ref,
                 sbq_ref, sbkt_ref, sbv_ref, mq_ref, mkt_ref, mv_ref):
    x = x_ref[0]
    hn = _rms(x, g_mix_ref[...]).astype(BF16)
    nat = _dot(hn, w_nat_ref[...])
    tr = _dot_nt(w_t_ref[...], hn)

    sb_scale = LOG2E / math.sqrt(HEAD_DIM)
    sbq_ref[0] = (nat[:, 0:GROUP_WIDTH] * sb_scale).astype(BF16)
    sbv_ref[0] = nat[:, GROUP_WIDTH:2 * GROUP_WIDTH].astype(BF16)
    sbkt_ref[0] = tr[0:GROUP_WIDTH, :].astype(BF16)

    q_lat = nat[:, 2 * GROUP_WIDTH:2 * GROUP_WIDTH + Q_LORA]
    kv_lat = nat[:, 2 * GROUP_WIDTH + Q_LORA:2 * GROUP_WIDTH + Q_LORA + KV_LORA]
    qn = _rms(q_lat, g_q_ref[...]).astype(BF16)
    kvn = _rms(kv_lat, g_kv_ref[...]).astype(BF16)

    ang = pos_col_ref[0].astype(F32) * invf_row_ref[...]
    cos_q, sin_q = jnp.cos(ang), jnp.sin(ang)
    ang_t = invf_col_ref[...] * pos_row_ref[0].astype(F32)
    cos_k, sin_k = jnp.cos(ang_t), jnp.sin(ang_t)

    q_all = _dot(qn, w_uq_ref[...])
    q_rot = _dot(qn, w_uq_rot_ref[...])
    m_scale = LOG2E / math.sqrt(HEAD_DIM + ROPE_DIM)
    for p in range(N_PAIRS):
        base = p * MLA_QK_WIDTH
        mq_ref[0, :, base:base + LANES] = (q_all[:, base:base + LANES] * m_scale).astype(BF16)
        pe = q_all[:, base + LANES:base + 2 * LANES] * cos_q + q_rot[:, p * LANES:(p + 1) * LANES] * sin_q
        mq_ref[0, :, base + LANES:base + 2 * LANES] = (pe * m_scale).astype(BF16)

    mv_ref[0] = _dot(kvn, w_uv_ref[...]).astype(BF16)
    kn_t = _dot_nt(w_uk_t_ref[...], kvn)
    kpe_t = (tr[GROUP_WIDTH:GROUP_WIDTH + LANES, :] * cos_k
             + tr[GROUP_WIDTH + LANES:GROUP_WIDTH + 2 * LANES, :] * sin_k).astype(BF16)
    for p in range(N_PAIRS):
        base = p * MLA_QK_WIDTH
        mkt_ref[0, base:base + LANES, :] = kn_t[p * LANES:(p + 1) * LANES, :].astype(BF16)
        mkt_ref[0, base + LANES:base + 2 * LANES, :] = kpe_t


def _sb_kernel(q_ref, kt_ref, v_ref, o_ref, acc_ref, c_ref, lbc_ref, hl_ref, att_ref):
    tq, tk = SB_Q_TILE, SB_K_TILE
    i = pl.program_id(2)
    n = (tq // tk) * (i + 1)
    q = q_ref[0]
    lane = lax.broadcasted_iota(jnp.int32, (tq, LANES), 1)
    zero = jnp.zeros_like(q)
    q_heads = (jnp.where(lane < HEAD_DIM, q, zero), jnp.where(lane >= HEAD_DIM, q, zero))
    krow = lax.broadcasted_iota(jnp.int32, (2 * tk, tk), 0)
    kcol = lax.broadcasted_iota(jnp.int32, (2 * tk, tk), 1)
    later2 = ((krow > kcol) & ((krow < tk) | (krow > kcol + tk))).astype(BF16)
    row = lax.broadcasted_iota(jnp.int32, (SB_ROW_CHUNK, tk), 0)
    col = lax.broadcasted_iota(jnp.int32, (SB_ROW_CHUNK, tk), 1)
    sign_bit = jnp.uint32(0x80000000)

    def kv_start(k):
        return pl.multiple_of((n - 1 - k) * tk, tk)

    def stage1_rows(z, k, slot, h, r0, first, diagonal):
        rows = slice(r0, r0 + SB_ROW_CHUNK)
        neg_abs = lax.bitcast_convert_type(lax.bitcast_convert_type(z, jnp.uint32) | sign_bit, F32)
        soft = jnp.log2(1.0 + jnp.exp2(neg_abs))
        log_beta = jnp.minimum(z, 0.0) - soft
        log_1m = log_beta - z
        if diagonal:
            causal = col + (tq - tk - k * tk - r0) < row
            log_1m = jnp.where(causal, log_1m, 0.0)
            log_beta = jnp.where(causal, log_beta, MASKED_LOG)
        tot = jnp.sum(log_1m, axis=1, keepdims=True)
        if first:
            lbc = log_beta
            c_ref[h, rows] = jnp.broadcast_to(tot, (SB_ROW_CHUNK, LANES))
        else:
            c = c_ref[h, rows]
            lbc = log_beta + _lane_tile(c, tk)
            c_ref[h, rows] = c + tot
        hi = log_1m.astype(BF16)
        lbc_ref[slot, h, rows] = lbc
        hl_ref[slot, h, rows, 0:tk] = hi
        hl_ref[slot, h, rows, tk:2 * tk] = (log_1m - hi.astype(F32)).astype(BF16)

    def step(k, slot, s1, s2, s3, first=False, diagonal=False):
        other = 1 - slot
        if s1:
            kt = kt_ref[0, :, pl.ds(kv_start(k), tk)]
        if s3:
            v = v_ref[0, pl.ds(kv_start(k - 2), tk), :]
        for b0 in range(0, tq, SB_MXU_ROWS):
            blk = slice(b0, b0 + SB_MXU_ROWS)
            if s1:
                zs = [_dot(q_heads[h][blk], kt) for h in range(2)]
            if s2:
                sufs = [_dot(hl_ref[other, h, blk], later2) for h in range(2)]
            if s3:
                pvs = [_dot(att_ref[slot, h, blk], v) for h in range(2)]
            for h in range(2):
                for r in range(0, SB_MXU_ROWS, SB_ROW_CHUNK):
                    if s1:
                        stage1_rows(zs[h][r:r + SB_ROW_CHUNK], k, slot, h, b0 + r, first, diagonal)
                    if s2:
                        rows = slice(b0 + r, b0 + r + SB_ROW_CHUNK)
                        arg = lbc_ref[other, h, rows] + sufs[h][r:r + SB_ROW_CHUNK]
                        att_ref[other, h, rows] = jnp.exp2(arg).astype(BF16)
                if s3:
                    acc_ref[h, blk] += pvs[h]

    acc_ref[...] = jnp.zeros_like(acc_ref)
    step(0, 0, True, False, False, first=True, diagonal=True)
    step(1, 1, True, True, False, diagonal=True)

    def body(m, carry):
        k = 2 + 2 * m
        step(k, 0, True, True, True)
        step(k + 1, 1, True, True, True)
        return carry

    lax.fori_loop(0, (n - 2) // 2, body, 0)
    step(n, 0, False, True, True)
    step(n + 1, 1, False, False, True)
    o_ref[0] = jnp.where(lane < HEAD_DIM, acc_ref[0], acc_ref[1])


def _mla_kernel(q_ref, kt_ref, v_ref, o_ref, acc_ref, m_ref, l_ref, p_ref, alpha_ref):
    t = MLA_TILE
    i = pl.program_id(2)
    n = i + 1
    q = q_ref[0]
    qlane = lax.broadcasted_iota(jnp.int32, (t, MLA_QK_WIDTH), 1)
    zero = jnp.zeros_like(q)
    pe0 = LANES
    sel0 = (qlane < HEAD_DIM) | ((qlane >= pe0) & (qlane < pe0 + ROPE_DIM))
    sel1 = ((qlane >= HEAD_DIM) & (qlane < pe0)) | ((qlane >= pe0 + ROPE_DIM) & (qlane < pe0 + 2 * ROPE_DIM))
    q_heads = (jnp.where(sel0, q, zero), jnp.where(sel1, q, zero))
    lane = lax.broadcasted_iota(jnp.int32, (t, LANES), 1)

    def kv_start(k):
        return pl.multiple_of((i - k) * t, t)

    def step(k, slot, s1, s2, diagonal=False):
        other = 1 - slot
        if s1:
            kt = kt_ref[0, :, pl.ds(kv_start(k), t)]
            ss = [_dot(q_heads[h], kt) for h in range(2)]
        if s2:
            v = v_ref[0, pl.ds(kv_start(k - 1), t), :]
            pvs = [_dot(p_ref[other, h], v) for h in range(2)]
        if s1:
            for h in range(2):
                s = ss[h]
                if diagonal:
                    row = lax.broadcasted_iota(jnp.int32, (t, t), 0)
                    col = lax.broadcasted_iota(jnp.int32, (t, t), 1)
                    s = jnp.where((col // CHUNK) <= (row // CHUNK), s, -jnp.inf)
                    m_new = jnp.broadcast_to(jnp.max(s, axis=1, keepdims=True), (t, LANES))
                else:
                    m_old = m_ref[h]
                    m_new = jnp.maximum(m_old, jnp.max(s, axis=1, keepdims=True))
                p = jnp.exp2(s - _lane_tile(m_new, t))
                psum = jnp.sum(p, axis=1, keepdims=True)
                p_ref[slot, h] = p.astype(BF16)
                if diagonal:
                    alpha_ref[slot, h] = jnp.ones((t, LANES), F32)
                    l_ref[h] = jnp.broadcast_to(psum, (t, LANES))
                else:
                    alpha = jnp.exp2(m_old - m_new)
                    alpha_ref[slot, h] = alpha
                    l_ref[h] = alpha * l_ref[h] + psum
                m_ref[h] = m_new
        if s2:
            for h in range(2):
                acc_ref[h] = alpha_ref[other, h] * acc_ref[h] + pvs[h]

    acc_ref[...] = jnp.zeros_like(acc_ref)
    step(0, 0, True, False, diagonal=True)

    def body(k, carry):
        _by_parity(k, lambda slot: step(k, slot, True, True))
        return carry

    lax.fori_loop(1, n, body, 0)
    _by_parity(n, lambda slot: step(n, slot, False, True))
    o_ref[0] = jnp.where(lane < HEAD_DIM, acc_ref[0] / l_ref[0], acc_ref[1] / l_ref[1])


def _epilogue_kernel(x_ref, sb_ref, mla_ref, g_sb_ref, g_mla_ref, w_out_ref, g_ffn_ref,
                     w_gate_ref, w_up_ref, w_down_ref, g_final_ref, o_ref):
    a = _rms(sb_ref[0], g_sb_ref[...]).astype(BF16)
    b = _rms(mla_ref[0], g_mla_ref[...]).astype(BF16)
    h = x_ref[0] + _dot(a, w_out_ref[0:GROUP_WIDTH, :]) + _dot(b, w_out_ref[GROUP_WIDTH:2 * GROUP_WIDTH, :])
    fn = _rms(h, g_ffn_ref[...]).astype(BF16)
    gate = _dot(fn, w_gate_ref[...])
    up = _dot(fn, w_up_ref[...])
    act = (gate / (1.0 + jnp.exp(-gate)) * up).astype(BF16)
    h = h + _dot(act, w_down_ref[...])
    o_ref[0] = _rms(h, g_final_ref[...])


def _const_spec(shape):
    return pl.BlockSpec(shape, lambda *_: (0,) * len(shape), pipeline_mode=pl.Buffered(1))


def _rot_cols(w):
    half = ROPE_DIM // 2
    return jnp.concatenate([-w[..., half:], w[..., :half]], axis=-1)


def kernel(x, positions, g_mix_norm, w_in, g_q_latent, w_uq, g_kv_latent, w_ukv,
           g_sb_out, g_mla_out, w_out, g_ffn_norm, w_gate, w_up, w_down, g_final):
    b, s, d = x.shape
    hidden = w_gate.shape[1]
    assert d == D_MODEL
    assert s % TOKEN_TILE == 0 and s % SB_Q_TILE == 0 and s % MLA_TILE == 0 and s % EPILOGUE_TILE == 0

    c0, c1, c2 = GROUP_WIDTH, 2 * GROUP_WIDTH, 3 * GROUP_WIDTH
    c3, c4 = c2 + Q_LORA, c2 + Q_LORA + KV_LORA
    w_nat = jnp.concatenate([w_in[:, 0:c0], w_in[:, c1:c2], w_in[:, c2:c3], w_in[:, c3:c4]], axis=1).astype(BF16)
    w_kr = w_in[:, c4:c4 + ROPE_DIM]
    pad = jnp.zeros((d, LANES - 2 * ROPE_DIM), F32)
    w_t = jnp.concatenate([w_in[:, c0:c1], w_kr, w_kr, pad, _rot_cols(w_kr), _rot_cols(w_kr), pad],
                          axis=1).T.astype(BF16)

    uq = w_uq.reshape(Q_LORA, N_HEADS, HEAD_DIM + ROPE_DIM)
    uq_nope = uq[:, :, :HEAD_DIM].reshape(Q_LORA, N_PAIRS, 2 * HEAD_DIM)
    uq_pe = uq[:, :, HEAD_DIM:].reshape(Q_LORA, N_PAIRS, 2 * ROPE_DIM)
    uq_rot = _rot_cols(uq[:, :, HEAD_DIM:]).reshape(Q_LORA, N_PAIRS, 2 * ROPE_DIM)
    qpad = jnp.zeros((Q_LORA, N_PAIRS, LANES - 2 * ROPE_DIM), F32)
    w_uq_ext = jnp.concatenate([uq_nope, uq_pe, qpad], axis=2).reshape(Q_LORA, N_PAIRS * MLA_QK_WIDTH).astype(BF16)
    w_uq_rot = jnp.concatenate([uq_rot, qpad], axis=2).reshape(Q_LORA, N_PAIRS * LANES).astype(BF16)

    ukv = w_ukv.reshape(KV_LORA, N_HEADS, 2 * HEAD_DIM)
    w_uk_t = ukv[:, :, :HEAD_DIM].reshape(KV_LORA, GROUP_WIDTH).T.astype(BF16)
    w_uv = ukv[:, :, HEAD_DIM:].reshape(KV_LORA, GROUP_WIDTH).astype(BF16)

    inv_freq = 1.0 / (ROPE_THETA ** (jnp.arange(0, ROPE_DIM, 2, dtype=F32) / ROPE_DIM))
    invf = jnp.concatenate([inv_freq] * 4 + [jnp.zeros((LANES - 2 * ROPE_DIM,), F32)])
    invf_row, invf_col = invf.reshape(1, LANES), invf.reshape(LANES, 1)

    pos_col = positions.reshape(b, s, 1)
    pos_row = positions.reshape(b, 1, s)
    row2 = lambda g: g.reshape(1, -1)

    tm = TOKEN_TILE
    tok3 = lambda width: pl.BlockSpec((1, tm, width), lambda bi, ti: (bi, ti, 0))
    tr3 = lambda rows: pl.BlockSpec((1, rows, tm), lambda bi, ti: (bi, 0, ti))
    sbq, sbkt, sbv, mq, mkt, mv = pl.pallas_call(
        _proj_kernel,
        grid=(b, s // tm),
        in_specs=[tok3(d), tok3(1), tr3(1), _const_spec((1, d)), _const_spec(w_nat.shape), _const_spec(w_t.shape),
                  _const_spec((1, Q_LORA)), _const_spec(w_uq_ext.shape), _const_spec(w_uq_rot.shape),
                  _const_spec((1, KV_LORA)), _const_spec(w_uv.shape), _const_spec(w_uk_t.shape),
                  _const_spec((1, LANES)), _const_spec((LANES, 1))],
        out_specs=[tok3(GROUP_WIDTH), tr3(GROUP_WIDTH), tok3(GROUP_WIDTH),
                   tok3(N_PAIRS * MLA_QK_WIDTH), tr3(N_PAIRS * MLA_QK_WIDTH), tok3(GROUP_WIDTH)],
        out_shape=[jax.ShapeDtypeStruct((b, s, GROUP_WIDTH), BF16),
                   jax.ShapeDtypeStruct((b, GROUP_WIDTH, s), BF16),
                   jax.ShapeDtypeStruct((b, s, GROUP_WIDTH), BF16),
                   jax.ShapeDtypeStruct((b, s, N_PAIRS * MLA_QK_WIDTH), BF16),
                   jax.ShapeDtypeStruct((b, N_PAIRS * MLA_QK_WIDTH, s), BF16),
                   jax.ShapeDtypeStruct((b, s, GROUP_WIDTH), BF16)],
        compiler_params=pltpu.CompilerParams(dimension_semantics=("arbitrary", "arbitrary"),
                                             vmem_limit_bytes=VMEM_LIMIT),
        name="projection",
    )(x, pos_col, pos_row, row2(g_mix_norm), w_nat, w_t, row2(g_q_latent), w_uq_ext, w_uq_rot,
      row2(g_kv_latent), w_uv, w_uk_t, invf_row, invf_col)

    att_params = pltpu.CompilerParams(dimension_semantics=("arbitrary", "arbitrary", "arbitrary"),
                                      vmem_limit_bytes=VMEM_LIMIT)
    q_spec = lambda t, width: pl.BlockSpec((1, t, width), lambda bi, pi, qi: (bi, qi, pi))
    kt_spec = lambda rows: pl.BlockSpec((1, rows, s), lambda bi, pi, qi: (bi, pi, 0))
    v_spec = pl.BlockSpec((1, s, LANES), lambda bi, pi, qi: (bi, 0, pi))
    o_spec = lambda t: pl.BlockSpec((1, t, LANES), lambda bi, pi, qi: (bi, qi, pi))
    att_out = jax.ShapeDtypeStruct((b, s, GROUP_WIDTH), F32)
    rows_f32 = lambda t: pltpu.VMEM((2, t, LANES), F32)
    slots = lambda tq, tk, dtype: pltpu.VMEM((2, 2, tq, tk), dtype)

    tq, tk = SB_Q_TILE, SB_K_TILE
    sb_out = pl.pallas_call(
        _sb_kernel,
        grid=(b, N_PAIRS, s // tq),
        in_specs=[q_spec(tq, LANES), kt_spec(LANES), v_spec],
        out_specs=o_spec(tq),
        out_shape=att_out,
        scratch_shapes=[rows_f32(tq), rows_f32(tq), slots(tq, tk, F32), slots(tq, 2 * tk, BF16),
                        slots(tq, tk, BF16)],
        compiler_params=att_params,
        name="stick_breaking_sweep",
    )(sbq, sbkt, sbv)

    t = MLA_TILE
    mla_out = pl.pallas_call(
        _mla_kernel,
        grid=(b, N_PAIRS, s // t),
        in_specs=[q_spec(t, MLA_QK_WIDTH), kt_spec(MLA_QK_WIDTH), v_spec],
        out_specs=o_spec(t),
        out_shape=att_out,
        scratch_shapes=[rows_f32(t), rows_f32(t), rows_f32(t), slots(t, t, BF16),
                        pltpu.VMEM((2, 2, t, LANES), F32)],
        compiler_params=att_params,
        name="latent_sweep",
    )(mq, mkt, mv)

    te = EPILOGUE_TILE
    etok = lambda width: pl.BlockSpec((1, te, width), lambda bi, ti: (bi, ti, 0))
    return pl.pallas_call(
        _epilogue_kernel,
        grid=(b, s // te),
        in_specs=[etok(d), etok(GROUP_WIDTH), etok(GROUP_WIDTH), _const_spec((1, GROUP_WIDTH)),
                  _const_spec((1, GROUP_WIDTH)), _const_spec((d, d)), _const_spec((1, d)),
                  _const_spec((d, hidden)), _const_spec((d, hidden)), _const_spec((hidden, d)),
                  _const_spec((1, d))],
        out_specs=etok(d),
        out_shape=jax.ShapeDtypeStruct((b, s, d), F32),
        compiler_params=pltpu.CompilerParams(dimension_semantics=("arbitrary", "arbitrary"),
                                             vmem_limit_bytes=VMEM_LIMIT),
        name="epilogue",
    )(x, sb_out, mla_out, row2(g_sb_out), row2(g_mla_out), w_out.astype(BF16), row2(g_ffn_norm),
      w_gate.astype(BF16), w_up.astype(BF16), w_down.astype(BF16), row2(g_final))
```
